```python
import math
import jax
import jax.numpy as jnp
from jax import lax
import numpy as np

D_MODEL = 1024
BATCH = 8
SEQ = 4096
DEPTH = 4

GRID_W = 64
CTX_LEN = 256
N_BRANCH = 4
BRANCH_W = 512
ATT_HEADS = 4
ATT_HEAD_DIM = 64
ATT_V_DIM = 2 * ATT_HEAD_DIM
ATT_SCALE = ATT_HEAD_DIM ** -0.5
Q_BLOCK = 128
ROPE_THETA = 10000.0
ROPE_AXIS_PAIRS = ATT_HEAD_DIM // 4
CHUNK = 128
SGU_GROUPS = 4
SGU_GROUP_W = BRANCH_W // SGU_GROUPS
SCONV_K = 3
CCONV_K = 31
D_FF = 4 * D_MODEL
LN_EPS = 1e-5
DEEPNORM_ALPHA = (2 * DEPTH) ** 0.25
DEEPNORM_BETA = (8 * DEPTH) ** -0.25

Q_W = 2 * ATT_HEADS * ATT_HEAD_DIM
K_W = 2 * ATT_HEADS * ATT_HEAD_DIM
V_W = ATT_HEADS * ATT_V_DIM
SGU_IN_W = 2 * BRANCH_W
SCONV_IN_W = 3 * BRANCH_W
CCONV_IN_W = 2 * BRANCH_W
GATE_W = N_BRANCH * D_MODEL
OFF_Q = 0
OFF_K = OFF_Q + Q_W
OFF_V = OFF_K + K_W
OFF_SGU = OFF_V + V_W
OFF_SCONV = OFF_SGU + SGU_IN_W
OFF_CCONV = OFF_SCONV + SCONV_IN_W
OFF_GATE = OFF_CCONV + CCONV_IN_W
N_IN = OFF_GATE + GATE_W

kernel_name = 'hybrid_gated_branch_dit_layers'


def layer_norm(x, g, b):
    xf = x.astype(jnp.float32)
    mu = jnp.mean(xf, axis=-1, keepdims=True)
    var = jnp.mean(jnp.square(xf - mu), axis=-1, keepdims=True)
    y = ((xf - mu) * lax.rsqrt(var + LN_EPS)).astype(x.dtype)
    return y * g + b


def rms_norm(x, g):
    xf = x.astype(jnp.float32)
    y = xf * lax.rsqrt(jnp.mean(xf * xf, axis=-1, keepdims=True) + LN_EPS)
    return y.astype(x.dtype) * g


def modulate(x, shift, scale):
    return x * (1 + scale) + shift


def post_norm(x, update, g, b):
    return layer_norm(DEEPNORM_ALPHA * x + update, g, b)


def depthwise_conv(x, w):
    return lax.conv_general_dilated(
        x, w[:, None, :], window_strides=(1,), padding='SAME',
        dimension_numbers=('NWC', 'WIO', 'NWC'), feature_group_count=x.shape[-1])


def axial_rope_tables(n_tokens):
    rows = n_tokens // GRID_W
    row = jnp.repeat(jnp.arange(rows, dtype=jnp.float32), GRID_W)
    col = jnp.tile(jnp.arange(GRID_W, dtype=jnp.float32), rows)
    inv_freq = ROPE_THETA ** (-jnp.arange(ROPE_AXIS_PAIRS, dtype=jnp.float32) / ROPE_AXIS_PAIRS)
    ang = jnp.concatenate([row[:, None] * inv_freq, col[:, None] * inv_freq], axis=-1)
    return jnp.cos(ang), jnp.sin(ang)


def apply_rope(x, cos, sin):
    xp = x.reshape(x.shape[:-1] + (ATT_HEAD_DIM // 2, 2))
    x0, x1 = xp[..., 0], xp[..., 1]
    cs = cos[:, None, None, :].astype(x.dtype)
    sn = sin[:, None, None, :].astype(x.dtype)
    return jnp.stack([x0 * cs - x1 * sn, x0 * sn + x1 * cs], axis=-1).reshape(x.shape)


def split_q(z):
    b, n = z.shape[:2]
    return z[..., OFF_Q:OFF_K].reshape(b, n, 2, ATT_HEADS, ATT_HEAD_DIM) * ATT_SCALE


def split_kv(zkv):
    b, n = zkv.shape[:2]
    k = zkv[..., :K_W].reshape(b, n, 2, ATT_HEADS, ATT_HEAD_DIM)
    v = zkv[..., K_W:].reshape(b, n, ATT_HEADS, ATT_V_DIM)
    return k, v


def diff_attn_block(q, k, v, lam):
    s = jnp.einsum('bqmhd,bkmhd->bmhqk', q, k).astype(jnp.float32)
    p = jax.nn.softmax(s, axis=-1)
    a = p[:, 0] - lam * p[:, 1]
    return jnp.einsum('bhqk,bkhe->bqhe', a.astype(v.dtype), v)


def diff_attn_latent(q, k, v, lam):
    b, n = q.shape[:2]
    nb = n // Q_BLOCK
    qb = jnp.moveaxis(q.reshape(b, nb, Q_BLOCK, 2, ATT_HEADS, ATT_HEAD_DIM), 1, 0)
    out = lax.map(lambda blk: diff_attn_block(blk, k, v, lam), qb)
    return jnp.moveaxis(out, 0, 1).reshape(b, n, ATT_HEADS, ATT_V_DIM)


def diff_attn_finish(o, g, lam_init):
    b, n = o.shape[:2]
    return (rms_norm(o, g) * (1.0 - lam_init)).reshape(b, n, V_W)


def chunk_sgu(z, ln_g, ln_b, w_s, b_s):
    b, n = z.shape[:2]
    u, v = jnp.split(jax.nn.gelu(z, approximate=False), 2, axis=-1)
    v = layer_norm(v, ln_g, ln_b)
    vb = v.reshape(b, n // CHUNK, CHUNK, SGU_GROUPS, SGU_GROUP_W)
    mixed = jnp.einsum('gpq,bnqgc->bnpgc', w_s, vb) + b_s.T[:, :, None]
    return u * mixed.reshape(b, n, BRANCH_W)


def short_conv(z, w):
    gb, gc, xt = jnp.split(z, 3, axis=-1)
    return gb * depthwise_conv(gc * xt, w)


def conformer_conv(z, w_dw, b_dw, ln_g, ln_b):
    y = jax.nn.glu(z, axis=-1)
    y = depthwise_conv(y, w_dw) + b_dw
    return jax.nn.silu(layer_norm(y, ln_g, ln_b))


def layer_update(x, z, y_b, gate1, shift2, scale2, gate2,
                 sgu_ln_g, sgu_ln_b, sgu_w, sgu_b, sconv_w, cconv_w, cconv_b,
                 cconv_ln_g, cconv_ln_b, w_branch, w_out, ln1_g, ln1_b,
                 w_up, w_down, ln2_g, ln2_b):
    y_a = chunk_sgu(z[..., OFF_SGU:OFF_SCONV], sgu_ln_g, sgu_ln_b, sgu_w, sgu_b)
    y_c = short_conv(z[..., OFF_SCONV:OFF_CCONV], sconv_w)
    y_d = conformer_conv(z[..., OFF_CCONV:OFF_GATE], cconv_w, cconv_b, cconv_ln_g, cconv_ln_b)
    branches = jnp.stack([y_a, y_b, y_c, y_d], axis=2)
    proj = jnp.einsum('bnkw,kwd->bnkd', branches, w_branch)
    gates = jax.nn.sigmoid(z[..., OFF_GATE:].reshape(z.shape[:2] + (N_BRANCH, D_MODEL)))
    mix = jnp.sum(gates * proj, axis=2) @ w_out
    x = post_norm(x, gate1 * mix, ln1_g, ln1_b)
    h = modulate(x, shift2, scale2)
    ffn = jnp.square(jax.nn.relu(h @ w_up)) @ w_down
    return post_norm(x, gate2 * ffn, ln2_g, ln2_b)


def setup_inputs(seed: int = 0) -> dict:
    key = jax.random.key(seed)
    ks = jax.random.split(key, 29)
    f32 = jnp.float32
    L = DEPTH

    def nrm(k, shape, scale):
        return jax.random.normal(k, shape, f32) * scale

    def gain(k, shape):
        return 1.0 + 0.02 * jax.random.normal(k, shape, f32)

    return {
        'x': nrm(ks[0], (BATCH, SEQ, D_MODEL), 1.0),
        'c': nrm(ks[1], (BATCH, D_MODEL), 1.0),
        'ctx': nrm(ks[2], (BATCH, CTX_LEN, D_MODEL), 1.0),
        'c_ctx': nrm(ks[3], (D_MODEL,), 1.0),
        'w_ada': nrm(ks[4], (L, D_MODEL, 6 * D_MODEL), 0.5 * D_MODEL ** -0.5),
        'b_ada': nrm(ks[5], (L, 6 * D_MODEL), 0.02),
        'w_in': nrm(ks[6], (L, D_MODEL, N_IN), D_MODEL ** -0.5),
        'lam_q1': nrm(ks[7], (L, ATT_HEAD_DIM), 0.1),
        'lam_k1': nrm(ks[8], (L, ATT_HEAD_DIM), 0.1),
        'lam_q2': nrm(ks[9], (L, ATT_HEAD_DIM), 0.1),
        'lam_k2': nrm(ks[10], (L, ATT_HEAD_DIM), 0.1),
        'attn_subln_g': gain(ks[11], (L, ATT_V_DIM)),
        'sgu_ln_g': gain(ks[12], (L, BRANCH_W)),
        'sgu_ln_b': nrm(ks[13], (L, BRANCH_W), 0.02),
        'sgu_w': nrm(ks[14], (L, SGU_GROUPS, CHUNK, CHUNK), CHUNK ** -0.5),
        'sgu_b': gain(ks[15], (L, SGU_GROUPS, CHUNK)),
        'sconv_w': nrm(ks[16], (L, SCONV_K, BRANCH_W), SCONV_K ** -0.5),
        'cconv_w': nrm(ks[17], (L, CCONV_K, BRANCH_W), CCONV_K ** -0.5),
        'cconv_b': nrm(ks[18], (L, BRANCH_W), 0.02),
        'cconv_ln_g': gain(ks[19], (L, BRANCH_W)),
        'cconv_ln_b': nrm(ks[20], (L, BRANCH_W), 0.02),
        'w_branch': nrm(ks[21], (L, N_BRANCH, BRANCH_W, D_MODEL), DEEPNORM_BETA * BRANCH_W ** -0.5),
        'w_out': nrm(ks[22], (L, D_MODEL, D_MODEL), DEEPNORM_BETA * D_MODEL ** -0.5),
        'ln1_g': gain(ks[23], (L, D_MODEL)),
        'ln1_b': nrm(ks[24], (L, D_MODEL), 0.02),
        'w_up': nrm(ks[25], (L, D_MODEL, D_FF), D_MODEL ** -0.5),
        'w_down': nrm(ks[26], (L, D_FF, D_MODEL), DEEPNORM_BETA * D_FF ** -0.5),
        'ln2_g': gain(ks[27], (L, D_MODEL)),
        'ln2_b': nrm(ks[28], (L, D_MODEL), 0.02),
    }


def reference(x, c, ctx, c_ctx, w_ada, b_ada, w_in, lam_q1, lam_k1, lam_q2, lam_k2,
              attn_subln_g, sgu_ln_g, sgu_ln_b, sgu_w, sgu_b, sconv_w, cconv_w, cconv_b,
              cconv_ln_g, cconv_ln_b, w_branch, w_out, ln1_g, ln1_b, w_up, w_down,
              ln2_g, ln2_b):
    n_lat = x.shape[1]
    cos, sin = axial_rope_tables(n_lat)
    c_act = jax.nn.silu(c)
    cctx_act = jax.nn.silu(c_ctx)
    xl, xc = x, ctx
    for l in range(DEPTH):
        lam_init = 0.8 - 0.6 * math.exp(-0.3 * l)
        lam = (jnp.exp(jnp.sum((lam_q1[l] * lam_k1[l]).astype(jnp.float32)))
               - jnp.exp(jnp.sum((lam_q2[l] * lam_k2[l]).astype(jnp.float32))) + lam_init)
        mod_lat = jnp.split((c_act @ w_ada[l] + b_ada[l])[:, None, :], 6, axis=-1)
        mod_ctx = jnp.split((cctx_act @ w_ada[l] + b_ada[l])[None, None, :], 6, axis=-1)
        lp = (sgu_ln_g[l], sgu_ln_b[l], sgu_w[l], sgu_b[l], sconv_w[l], cconv_w[l], cconv_b[l],
              cconv_ln_g[l], cconv_ln_b[l], w_branch[l], w_out[l], ln1_g[l], ln1_b[l],
              w_up[l], w_down[l], ln2_g[l], ln2_b[l])

        hc = modulate(xc, mod_ctx[0], mod_ctx[1])
        if l == DEPTH - 1:
            kc, vc = split_kv(hc @ w_in[l, :, OFF_K:OFF_SGU])
            xc_next = xc
        else:
            zc = hc @ w_in[l]
            qc = split_q(zc)
            kc, vc = split_kv(zc[..., OFF_K:OFF_SGU])
            yb_c = diff_attn_finish(diff_attn_block(qc, kc, vc, lam), attn_subln_g[l], lam_init)
            xc_next = layer_update(xc, zc, yb_c, *mod_ctx[2:], *lp)

        hl = modulate(xl, mod_lat[0], mod_lat[1])
        zl = hl @ w_in[l]
        ql = apply_rope(split_q(zl), cos, sin)
        kl, vl = split_kv(zl[..., OFF_K:OFF_SGU])
        kl = apply_rope(kl, cos, sin)
        k_all = jnp.concatenate([kc, kl], axis=1)
        v_all = jnp.concatenate([vc, vl], axis=1)
        yb_l = diff_attn_finish(diff_attn_latent(ql, k_all, v_all, lam), attn_subln_g[l], lam_init)
        xl = layer_update(xl, zl, yb_l, *mod_lat[2:], *lp)
        xc = xc_next
    return xl
```

```python
import functools
import math

import jax
import jax.numpy as jnp
import numpy as np
from jax import lax
from jax.experimental import pallas as pl
from jax.experimental.pallas import tpu as pltpu

D_MODEL = 1024
DEPTH = 4
GRID_W = 64
N_BRANCH = 4
BRANCH_W = 512
ATT_HEADS = 4
ATT_HEAD_DIM = 64
ATT_V_DIM = 2 * ATT_HEAD_DIM
ATT_SCALE = ATT_HEAD_DIM ** -0.5
ROPE_THETA = 10000.0
ROPE_AXIS_PAIRS = ATT_HEAD_DIM // 4
CHUNK = 128
SGU_GROUPS = 4
SCONV_K = 3
CCONV_K = 31
D_FF = 4 * D_MODEL
LN_EPS = 1e-5
DEEPNORM_ALPHA = (2 * DEPTH) ** 0.25

QKV_W = 3 * ATT_HEADS * ATT_V_DIM
HEAD_W = ATT_V_DIM
N_IN = 9216
REST_W = N_IN - QKV_W
REST_TN = REST_W // 4
R_GC, R_XT, R_GA, R_GB, R_SB, R_SU, R_SV, R_GATE = 0, 512, 1024, 1536, 2048, 2560, 3072, 3584
HALO_W = 2048
HALO_ROWS = 16
CCONV_PAD = (CCONV_K - 1) // 2

VMEM_LIMIT = 56 * 1024 * 1024

BF16 = jnp.bfloat16
F32 = jnp.float32


def _cparams(sem):
    return pltpu.CompilerParams(dimension_semantics=sem, vmem_limit_bytes=VMEM_LIMIT)


def _dot(a, b):
    return jnp.dot(a, b, preferred_element_type=F32)


def _dot_nt(a, b):
    return lax.dot_general(a, b, (((1,), (1,)), ((), ())), preferred_element_type=F32)


def _layer_norm(x, g, b):
    mu = jnp.mean(x, axis=-1, keepdims=True)
    xc = x - mu
    var = jnp.mean(xc * xc, axis=-1, keepdims=True)
    return xc * lax.rsqrt(var + LN_EPS) * g + b


def _sigmoid(x):
    return 1.0 / (1.0 + jnp.exp(-x))


def _ada_kernel(c_ref, w_ref, b_ref, o_ref):
    c = c_ref[...]
    act = (c * _sigmoid(c)).astype(BF16)
    o_ref[0] = _dot(act, w_ref[0].astype(BF16)) + b_ref[0]


def _ada_call(cc, w_ada, b_ada):
    L = w_ada.shape[0]
    rows = cc.shape[0]
    tn = 1536
    return pl.pallas_call(
        _ada_kernel,
        grid=(L, 6 * D_MODEL // tn),
        in_specs=[
            pl.BlockSpec((rows, D_MODEL), lambda l, j: (0, 0)),
            pl.BlockSpec((1, D_MODEL, tn), lambda l, j: (l, 0, j)),
            pl.BlockSpec((1, 1, tn), lambda l, j: (l, 0, j)),
        ],
        out_specs=pl.BlockSpec((1, rows, tn), lambda l, j: (l, 0, j)),
        out_shape=jax.ShapeDtypeStruct((L, rows, 6 * D_MODEL), F32),
        compiler_params=_cparams(("parallel", "parallel")),
        name="ada_mod",
    )(cc, w_ada, b_ada.reshape(L, 1, 6 * D_MODEL))


def _qkv_kernel(x_ref, shift_ref, scale_ref, w_ref, *rest, use_rope):
    if use_rope:
        cos_ref, sin_ref, q_ref, k1_ref, k2_ref, vt_ref = rest
    else:
        q_ref, k1_ref, k2_ref, vt_ref = rest
    h = (x_ref[0] * (1.0 + scale_ref[0]) + shift_ref[0]).astype(BF16)
    z = _dot(h, w_ref[...])
    t = z.shape[0]
    lane = lax.broadcasted_iota(jnp.int32, (t, HEAD_W), 1)
    map0 = (lane & 32) == 0
    if use_rope:
        cos = cos_ref[...]
        sin = sin_ref[...]
    for hd in range(ATT_HEADS):
        qh = z[:, hd * HEAD_W:(hd + 1) * HEAD_W] * ATT_SCALE
        kh = z[:, 512 + hd * HEAD_W:512 + (hd + 1) * HEAD_W]
        vh = z[:, 1024 + hd * HEAD_W:1024 + (hd + 1) * HEAD_W]
        if use_rope:
            qh = qh * cos + pltpu.roll(qh, 64, 1) * sin
            kh = kh * cos + pltpu.roll(kh, 64, 1) * sin
        q_ref[0, :, hd * HEAD_W:(hd + 1) * HEAD_W] = qh.astype(BF16)
        k1_ref[0, :, hd * HEAD_W:(hd + 1) * HEAD_W] = jnp.where(map0, kh, 0.0).astype(BF16)
        k2_ref[0, :, hd * HEAD_W:(hd + 1) * HEAD_W] = jnp.where(map0, 0.0, kh).astype(BF16)
        vt_ref[0, hd * HEAD_W:(hd + 1) * HEAD_W, :] = vh.T.astype(BF16)


def _qkv_call(x, shift, scale, w_qkv, cos, sin, tile):
    bsz, n, _ = x.shape
    use_rope = cos is not None
    vec = pl.BlockSpec((1, 1, D_MODEL), lambda b, i: (b, 0, 0))
    in_specs = [
        pl.BlockSpec((1, tile, D_MODEL), lambda b, i: (b, i, 0)),
        vec, vec,
        pl.BlockSpec((D_MODEL, QKV_W), lambda b, i: (0, 0)),
    ]
    args = [x, shift, scale, w_qkv]
    if use_rope:
        tab = pl.BlockSpec((tile, HEAD_W), lambda b, i: (i, 0))
        in_specs += [tab, tab]
        args += [cos, sin]
    row = pl.BlockSpec((1, tile, 512), lambda b, i: (b, i, 0))
    return pl.pallas_call(
        functools.partial(_qkv_kernel, use_rope=use_rope),
        grid=(bsz, n // tile),
        in_specs=in_specs,
        out_specs=[row, row, row, pl.BlockSpec((1, 512, tile), lambda b, i: (b, 0, i))],
        out_shape=[
            jax.ShapeDtypeStruct((bsz, n, 512), BF16),
            jax.ShapeDtypeStruct((bsz, n, 512), BF16),
            jax.ShapeDtypeStruct((bsz, n, 512), BF16),
            jax.ShapeDtypeStruct((bsz, 512, n), BF16),
        ],
        compiler_params=_cparams(("parallel", "parallel")),
        name="qkv_rope" if use_rope else "qkv_ctx",
    )(*args)


def _rest_kernel(x_ref, shift_ref, scale_ref, w_ref, o_ref):
    h = (x_ref[0] * (1.0 + scale_ref[0]) + shift_ref[0]).astype(BF16)
    o_ref[0] = _dot(h, w_ref[...]).astype(o_ref.dtype)


def _rest_call(x, shift, scale, w_rest, tile):
    bsz, n, _ = x.shape
    tn = REST_TN
    vec = pl.BlockSpec((1, 1, D_MODEL), lambda j, b, i: (b, 0, 0))
    return pl.pallas_call(
        _rest_kernel,
        grid=(REST_W // tn, bsz, n // tile),
        in_specs=[
            pl.BlockSpec((1, tile, D_MODEL), lambda j, b, i: (b, i, 0)),
            vec, vec,
            pl.BlockSpec((D_MODEL, tn), lambda j, b, i: (0, j)),
        ],
        out_specs=pl.BlockSpec((1, tile, tn), lambda j, b, i: (b, i, j)),
        out_shape=jax.ShapeDtypeStruct((bsz, n, REST_W), BF16),
        compiler_params=_cparams(("parallel", "parallel", "parallel")),
        name="in_proj_rest",
    )(x, shift, scale, w_rest)


def _attn_kernel(lam_ref, g_ref, q_ref, k1_ref, k2_ref, vt_ref, o_ref):
    lv = lam_ref[...]
    lam_init = lv[4:5, 0:1]
    lam = (jnp.exp(jnp.sum(lv[0:1] * lv[1:2], axis=-1, keepdims=True))
           - jnp.exp(jnp.sum(lv[2:3] * lv[3:4], axis=-1, keepdims=True)) + lam_init)
    q = q_ref[0]
    s1 = _dot_nt(k1_ref[0], q)
    s2 = _dot_nt(k2_ref[0], q)
    p1 = jnp.exp(s1 - jnp.max(s1, axis=0, keepdims=True))
    p2 = jnp.exp(s2 - jnp.max(s2, axis=0, keepdims=True))
    r1 = 1.0 / jnp.sum(p1, axis=0, keepdims=True)
    r2 = lam / jnp.sum(p2, axis=0, keepdims=True)
    a = (p1 * r1 - p2 * r2).astype(BF16)
    o = _dot(vt_ref[0], a).T
    y = o * lax.rsqrt(jnp.mean(o * o, axis=-1, keepdims=True) + LN_EPS)
    o_ref[0] = (y * g_ref[...] * (1.0 - lam_init)).astype(o_ref.dtype)


def _attn_call(lam_vecs, g, q, k1, k2, vt, tq):
    bsz, n, _ = q.shape
    nk = k1.shape[1]
    kspec = pl.BlockSpec((1, nk, HEAD_W), lambda b, h, i: (b, 0, h))
    return pl.pallas_call(
        _attn_kernel,
        grid=(bsz, ATT_HEADS, n // tq),
        in_specs=[
            pl.BlockSpec((8, ATT_HEAD_DIM), lambda b, h, i: (0, 0)),
            pl.BlockSpec((1, HEAD_W), lambda b, h, i: (0, 0)),
            pl.BlockSpec((1, tq, HEAD_W), lambda b, h, i: (b, i, h)),
            kspec, kspec,
            pl.BlockSpec((1, HEAD_W, nk), lambda b, h, i: (b, h, 0)),
        ],
        out_specs=pl.BlockSpec((1, tq, HEAD_W), lambda b, h, i: (b, i, h)),
        out_shape=jax.ShapeDtypeStruct((bsz, n, 512), BF16),
        compiler_params=_cparams(("parallel", "parallel", "parallel")),
        name="diff_attn",
    )(lam_vecs, g, q, k1, k2, vt)


def _mixer_kernel(z_ref, prev_ref, next_ref, yb_ref, x_ref, gate1_ref,
                  sgu_g_ref, sgu_b_ref, sgu_w_ref, sgu_bias_ref,
                  sconv_w_ref, cconv_w_ref, cconv_b_ref, cc_g_ref, cc_b_ref,
                  wbr_ref, wout_ref, ln_g_ref, ln_b_ref,
                  o_ref, u_buf, g_buf):
    i = pl.program_id(1)
    last = pl.num_programs(1) - 1
    t = x_ref.shape[1]
    has_prev = jnp.where(i > 0, 1.0, 0.0).astype(F32)
    has_next = jnp.where(i < last, 1.0, 0.0).astype(F32)

    def zs(col, width=BRANCH_W):
        return z_ref[0, :, col:col + width].astype(F32)

    def halo(ref, col):
        return ref[0, :, col:col + BRANCH_W].astype(F32)

    def glu(a, b):
        return a * _sigmoid(b)

    u_buf[0:HALO_ROWS, :] = halo(prev_ref, R_GC) * halo(prev_ref, R_XT) * has_prev
    u_buf[HALO_ROWS:HALO_ROWS + t, :] = zs(R_GC) * zs(R_XT)
    u_buf[HALO_ROWS + t:, :] = halo(next_ref, R_GC) * halo(next_ref, R_XT) * has_next
    conv = jnp.zeros((t, BRANCH_W), F32)
    for k in range(SCONV_K):
        off = HALO_ROWS - (SCONV_K - 1) // 2 + k
        conv = conv + u_buf[off:off + t, :] * sconv_w_ref[k:k + 1, :]
    y_c = zs(R_SB) * conv

    g_buf[0:HALO_ROWS, :] = glu(halo(prev_ref, R_GA), halo(prev_ref, R_GB)) * has_prev
    g_buf[HALO_ROWS:HALO_ROWS + t, :] = glu(zs(R_GA), zs(R_GB))
    g_buf[HALO_ROWS + t:, :] = glu(halo(next_ref, R_GA), halo(next_ref, R_GB)) * has_next
    conv = jnp.zeros((t, BRANCH_W), F32)
    for k in range(CCONV_K):
        off = HALO_ROWS - CCONV_PAD + k
        conv = conv + g_buf[off:off + t, :] * cconv_w_ref[k:k + 1, :]
    yd = _layer_norm(conv + cconv_b_ref[...], cc_g_ref[...], cc_b_ref[...])
    y_d = yd * _sigmoid(yd)

    def gelu(v):
        return 0.5 * v * (1.0 + lax.erf(v * np.float32(math.sqrt(0.5))))

    su = gelu(zs(R_SU))
    sv = _layer_norm(gelu(zs(R_SV)), sgu_g_ref[...], sgu_b_ref[...]).astype(BF16)
    rows = []
    for c in range(t // CHUNK):
        cols = []
        for g in range(SGU_GROUPS):
            vb = sv[c * CHUNK:(c + 1) * CHUNK, g * CHUNK:(g + 1) * CHUNK]
            cols.append(_dot(sgu_w_ref[g], vb))
        rows.append(jnp.concatenate(cols, axis=1) + sgu_bias_ref[...])
    y_a = su * jnp.concatenate(rows, axis=0)

    branches = (y_a.astype(BF16), yb_ref[0], y_c.astype(BF16), y_d.astype(BF16))
    mix = jnp.zeros((t, D_MODEL), F32)
    for k in range(N_BRANCH):
        gate = _sigmoid(zs(R_GATE + k * D_MODEL, D_MODEL))
        mix = mix + gate * _dot(branches[k], wbr_ref[k])
    upd = _dot(mix.astype(BF16), wout_ref[...])
    o_ref[0] = _layer_norm(DEEPNORM_ALPHA * x_ref[0] + gate1_ref[0] * upd,
                           ln_g_ref[...], ln_b_ref[...])


def _mixer_call(z, yb, x, gate1, lp, tile):
    bsz, n, _ = x.shape
    nh = n // HALO_ROWS
    hb = tile // HALO_ROWS

    def const(shape):
        return pl.BlockSpec(shape, lambda b, i: (0,) * len(shape))

    in_specs = [
        pl.BlockSpec((1, tile, REST_W), lambda b, i: (b, i, 0)),
        pl.BlockSpec((1, HALO_ROWS, HALO_W), lambda b, i: (b, jnp.maximum(i * hb - 1, 0), 0)),
        pl.BlockSpec((1, HALO_ROWS, HALO_W), lambda b, i: (b, jnp.minimum((i + 1) * hb, nh - 1), 0)),
        pl.BlockSpec((1, tile, BRANCH_W), lambda b, i: (b, i, 0)),
        pl.BlockSpec((1, tile, D_MODEL), lambda b, i: (b, i, 0)),
        pl.BlockSpec((1, 1, D_MODEL), lambda b, i: (b, 0, 0)),
        const((1, BRANCH_W)), const((1, BRANCH_W)),
        const((SGU_GROUPS, CHUNK, CHUNK)), const((CHUNK, BRANCH_W)),
        const((8, BRANCH_W)), const((32, BRANCH_W)),
        const((1, BRANCH_W)), const((1, BRANCH_W)), const((1, BRANCH_W)),
        const((N_BRANCH, BRANCH_W, D_MODEL)), const((D_MODEL, D_MODEL)),
        const((1, D_MODEL)), const((1, D_MODEL)),
    ]
    return pl.pallas_call(
        _mixer_kernel,
        grid=(bsz, n // tile),
        in_specs=in_specs,
        out_specs=pl.BlockSpec((1, tile, D_MODEL), lambda b, i: (b, i, 0)),
        out_shape=jax.ShapeDtypeStruct((bsz, n, D_MODEL), F32),
        scratch_shapes=[pltpu.VMEM((tile + 2 * HALO_ROWS, BRANCH_W), F32),
                        pltpu.VMEM((tile + 2 * HALO_ROWS, BRANCH_W), F32)],
        compiler_params=_cparams(("parallel", "parallel")),
        name="mixer",
    )(z, z, z, yb, x, gate1,
      lp["sgu_ln_g"], lp["sgu_ln_b"], lp["sgu_w"], lp["sgu_bias"],
      lp["sconv_w"], lp["cconv_w"], lp["cconv_b"], lp["cconv_ln_g"], lp["cconv_ln_b"],
      lp["w_branch"], lp["w_out"], lp["ln1_g"], lp["ln1_b"])


def _ffn_kernel(x_ref, shift_ref, scale_ref, gate_ref, wup_ref, wdn_ref, ln_g_ref, ln_b_ref, o_ref):
    x = x_ref[0]
    h = (x * (1.0 + scale_ref[0]) + shift_ref[0]).astype(BF16)
    acc = jnp.zeros(x.shape, F32)
    step = D_MODEL
    for c in range(D_FF // step):
        up = jnp.maximum(_dot(h, wup_ref[:, c * step:(c + 1) * step]), 0.0)
        acc = acc + _dot((up * up).astype(BF16), wdn_ref[c * step:(c + 1) * step, :])
    o_ref[0] = _layer_norm(DEEPNORM_ALPHA * x + gate_ref[0] * acc, ln_g_ref[...], ln_b_ref[...])


def _ffn_call(x, shift, scale, gate, lp, tile):
    bsz, n, _ = x.shape
    vec = pl.BlockSpec((1, 1, D_MODEL), lambda b, i: (b, 0, 0))
    row = pl.BlockSpec((1, tile, D_MODEL), lambda b, i: (b, i, 0))
    cvec = pl.BlockSpec((1, D_MODEL), lambda b, i: (0, 0))
    return pl.pallas_call(
        _ffn_kernel,
        grid=(bsz, n // tile),
        in_specs=[row, vec, vec, vec,
                  pl.BlockSpec((D_MODEL, D_FF), lambda b, i: (0, 0)),
                  pl.BlockSpec((D_FF, D_MODEL), lambda b, i: (0, 0)),
                  cvec, cvec],
        out_specs=row,
        out_shape=jax.ShapeDtypeStruct((bsz, n, D_MODEL), F32),
        compiler_params=_cparams(("parallel", "parallel")),
        name="ffn",
    )(x, shift, scale, gate, lp["w_up"], lp["w_down"], lp["ln2_g"], lp["ln2_b"])


def _qk_perm(offset):
    idx = np.empty((ATT_HEADS, 2, 2, 32), np.int32)
    for hd in range(ATT_HEADS):
        for r in range(2):
            for m in range(2):
                for p in range(32):
                    idx[hd, r, m, p] = offset + m * 256 + hd * ATT_HEAD_DIM + 2 * p + r
    return idx.reshape(-1)


_OFF_Q, _OFF_K, _OFF_V, _OFF_SGU, _OFF_SCONV, _OFF_CCONV, _OFF_GATE = 0, 512, 1024, 1536, 2560, 4096, 5120
_QKV_COLS = np.concatenate([_qk_perm(_OFF_Q), _qk_perm(_OFF_K), np.arange(_OFF_V, _OFF_V + 512)])
_REST_COLS = np.concatenate([
    np.arange(_OFF_SCONV + 512, _OFF_SCONV + 1536),
    np.arange(_OFF_CCONV, _OFF_CCONV + 1024),
    np.arange(_OFF_SCONV, _OFF_SCONV + 512),
    np.arange(_OFF_SGU, _OFF_SGU + 1024),
    np.arange(_OFF_GATE, _OFF_GATE + 4096),
])


def _rope_tables(n):
    rows = n // GRID_W
    row = jnp.repeat(jnp.arange(rows, dtype=F32), GRID_W)
    col = jnp.tile(jnp.arange(GRID_W, dtype=F32), rows)
    inv_freq = ROPE_THETA ** (-jnp.arange(ROPE_AXIS_PAIRS, dtype=F32) / ROPE_AXIS_PAIRS)
    ang = jnp.concatenate([row[:, None] * inv_freq, col[:, None] * inv_freq], axis=-1)
    cos, sin = jnp.cos(ang), jnp.sin(ang)
    return jnp.tile(cos, (1, 4)), jnp.concatenate([-sin, -sin, sin, sin], axis=-1)


def _pad_rows(w, rows):
    return jnp.pad(w, ((0, rows - w.shape[0]), (0, 0)))


def kernel(x, c, ctx, c_ctx, w_ada, b_ada, w_in, lam_q1, lam_k1, lam_q2, lam_k2, attn_subln_g, sgu_ln_g, sgu_ln_b, sgu_w, sgu_b, sconv_w, cconv_w, cconv_b, cconv_ln_g, cconv_ln_b, w_branch, w_out, ln1_g, ln1_b, w_up, w_down, ln2_g, ln2_b):
    bsz, n_lat, _ = x.shape
    n_ctx = ctx.shape[1]
    depth = w_in.shape[0]
    cos, sin = _rope_tables(n_lat)

    rows = ((bsz + 1 + 7) // 8) * 8
    cc = jnp.zeros((rows, D_MODEL), F32).at[:bsz].set(c).at[bsz].set(c_ctx)
    mod = _ada_call(cc, w_ada, b_ada)

    lat_tile = min(512, n_lat)
    tq = min(256, n_lat)
    xl = x
    xc = ctx.reshape(1, bsz * n_ctx, D_MODEL)
    for l in range(depth):
        lam_init = 0.8 - 0.6 * math.exp(-0.3 * l)
        lam_vecs = jnp.concatenate(
            [lam_q1[l][None], lam_k1[l][None], lam_q2[l][None], lam_k2[l][None],
             jnp.full((4, ATT_HEAD_DIM), lam_init, F32)], axis=0)
        g_attn = attn_subln_g[l][None]
        ml = [m[:, None, :] for m in jnp.split(mod[l, :bsz], 6, axis=-1)]
        mc = [m[None] for m in jnp.split(mod[l, bsz:bsz + 1], 6, axis=-1)]
        w_l = w_in[l].astype(BF16)
        w_qkv = w_l[:, _QKV_COLS]
        w_rest = w_l[:, _REST_COLS]
        lp = dict(
            sgu_ln_g=sgu_ln_g[l][None], sgu_ln_b=sgu_ln_b[l][None],
            sgu_w=sgu_w[l].astype(BF16),
            sgu_bias=jnp.repeat(sgu_b[l].T, CHUNK, axis=1),
            sconv_w=_pad_rows(sconv_w[l], 8), cconv_w=_pad_rows(cconv_w[l], 32),
            cconv_b=cconv_b[l][None], cconv_ln_g=cconv_ln_g[l][None], cconv_ln_b=cconv_ln_b[l][None],
            w_branch=w_branch[l].astype(BF16), w_out=w_out[l].astype(BF16),
            ln1_g=ln1_g[l][None], ln1_b=ln1_b[l][None],
            w_up=w_up[l].astype(BF16), w_down=w_down[l].astype(BF16),
            ln2_g=ln2_g[l][None], ln2_b=ln2_b[l][None],
        )
        last = l == depth - 1

        qc, k1c, k2c, vtc = _qkv_call(xc, mc[0], mc[1], w_qkv, None, None, min(1024, bsz * n_ctx))
        k1c = k1c.reshape(bsz, n_ctx, 512)
        k2c = k2c.reshape(bsz, n_ctx, 512)
        vtc = vtc.reshape(512, bsz, n_ctx).transpose(1, 0, 2)
        if not last:
            zc = _rest_call(xc, mc[0], mc[1], w_rest, min(1024, bsz * n_ctx))
            ybc = _attn_call(lam_vecs, g_attn, qc.reshape(bsz, n_ctx, 512), k1c, k2c, vtc, n_ctx)
            xcb = xc.reshape(bsz, n_ctx, D_MODEL)
            gate1c = jnp.broadcast_to(mc[2], (bsz, 1, D_MODEL))
            xm = _mixer_call(zc.reshape(bsz, n_ctx, REST_W), ybc, xcb, gate1c, lp, n_ctx)
            xc_next = _ffn_call(xm.reshape(1, bsz * n_ctx, D_MODEL), mc[3], mc[4], mc[5], lp,
                                min(512, bsz * n_ctx))

        ql, k1l, k2l, vtl = _qkv_call(xl, ml[0], ml[1], w_qkv, cos, sin, lat_tile)
        zl = _rest_call(xl, ml[0], ml[1], w_rest, min(1024, n_lat))
        k1 = jnp.concatenate([k1c, k1l], axis=1)
        k2 = jnp.concatenate([k2c, k2l], axis=1)
        vt = jnp.concatenate([vtc, vtl], axis=2)
        ybl = _attn_call(lam_vecs, g_attn, ql, k1, k2, vt, tq)
        xm = _mixer_call(zl, ybl, xl, ml[2], lp, lat_tile)
        xl = _ffn_call(xm, ml[3], ml[4], ml[5], lp, lat_tile)
        if not last:
            xc = xc_next
    return xl
```

```python
import functools
import math

import jax
import jax.numpy as jnp
import numpy as np
from jax import lax
from jax.experimental import pallas as pl
from jax.experimental.pallas import tpu as pltpu

D_MODEL = 1024
DEPTH = 4
GRID_W = 64
N_BRANCH = 4
BRANCH_W = 512
ATT_HEADS = 4
ATT_HEAD_DIM = 64
ATT_V_DIM = 2 * ATT_HEAD_DIM
ATT_SCALE = ATT_HEAD_DIM ** -0.5
LOG2_E = math.log2(math.e)
ROPE_THETA = 10000.0
ROPE_AXIS_PAIRS = ATT_HEAD_DIM // 4
CHUNK = 128
SGU_GROUPS = 4
SCONV_K = 3
CCONV_K = 31
D_FF = 4 * D_MODEL
LN_EPS = 1e-5
DEEPNORM_ALPHA = (2 * DEPTH) ** 0.25

QKV_W = 3 * ATT_HEADS * ATT_V_DIM
HEAD_W = ATT_V_DIM
N_IN = 9216
REST_W = N_IN - QKV_W
REST_TN = REST_W // 4
R_GC, R_XT, R_GA, R_GB, R_SB, R_SU, R_SV, R_GATE = 0, 512, 1024, 1536, 2048, 2560, 3072, 3584
HALO_W = 2048
HALO_ROWS = 16
CCONV_PAD = (CCONV_K - 1) // 2

VMEM_LIMIT = 56 * 1024 * 1024

BF16 = jnp.bfloat16
F32 = jnp.float32


def _cparams(sem):
    return pltpu.CompilerParams(dimension_semantics=sem, vmem_limit_bytes=VMEM_LIMIT)


def _dot(a, b):
    return jnp.dot(a, b, preferred_element_type=F32)


def _dot_nt(a, b):
    return lax.dot_general(a, b, (((1,), (1,)), ((), ())), preferred_element_type=F32)


def _layer_norm(x, g, b):
    mu = jnp.mean(x, axis=-1, keepdims=True)
    xc = x - mu
    var = jnp.mean(xc * xc, axis=-1, keepdims=True)
    return xc * lax.rsqrt(var + LN_EPS) * g + b


def _sigmoid(x):
    return 1.0 / (1.0 + jnp.exp(-x))


def _ada_kernel(c_ref, w_ref, b_ref, o_ref):
    c = c_ref[...]
    act = (c * _sigmoid(c)).astype(BF16)
    o_ref[0] = _dot(act, w_ref[0].astype(BF16)) + b_ref[0]


def _ada_call(cc, w_ada, b_ada):
    L = w_ada.shape[0]
    rows = cc.shape[0]
    tn = 1536
    return pl.pallas_call(
        _ada_kernel,
        grid=(L, 6 * D_MODEL // tn),
        in_specs=[
            pl.BlockSpec((rows, D_MODEL), lambda l, j: (0, 0)),
            pl.BlockSpec((1, D_MODEL, tn), lambda l, j: (l, 0, j)),
            pl.BlockSpec((1, 1, tn), lambda l, j: (l, 0, j)),
        ],
        out_specs=pl.BlockSpec((1, rows, tn), lambda l, j: (l, 0, j)),
        out_shape=jax.ShapeDtypeStruct((L, rows, 6 * D_MODEL), F32),
        compiler_params=_cparams(("parallel", "parallel")),
        name="ada_mod",
    )(cc, w_ada, b_ada.reshape(L, 1, 6 * D_MODEL))


def _qkv_kernel(x_ref, shift_ref, scale_ref, w_ref, *rest, use_rope):
    if use_rope:
        cos_ref, sin_ref, q_ref, k1_ref, k2_ref, vt_ref = rest
    else:
        q_ref, k1_ref, k2_ref, vt_ref = rest
    h = (x_ref[0] * (1.0 + scale_ref[0]) + shift_ref[0]).astype(BF16)
    z = _dot(h, w_ref[...])
    t = z.shape[0]
    lane = lax.broadcasted_iota(jnp.int32, (t, HEAD_W), 1)
    map0 = (lane & 32) == 0
    if use_rope:
        cos = cos_ref[...]
        sin = sin_ref[...]
    for hd in range(ATT_HEADS):
        qh = z[:, hd * HEAD_W:(hd + 1) * HEAD_W] * (ATT_SCALE * LOG2_E)
        kh = z[:, 512 + hd * HEAD_W:512 + (hd + 1) * HEAD_W]
        vh = z[:, 1024 + hd * HEAD_W:1024 + (hd + 1) * HEAD_W]
        if use_rope:
            qh = qh * cos + pltpu.roll(qh, 64, 1) * sin
            kh = kh * cos + pltpu.roll(kh, 64, 1) * sin
        q_ref[0, :, hd * HEAD_W:(hd + 1) * HEAD_W] = qh.astype(BF16)
        k1_ref[0, :, hd * HEAD_W:(hd + 1) * HEAD_W] = jnp.where(map0, kh, 0.0).astype(BF16)
        k2_ref[0, :, hd * HEAD_W:(hd + 1) * HEAD_W] = jnp.where(map0, 0.0, kh).astype(BF16)
        vt_ref[0, hd * HEAD_W:(hd + 1) * HEAD_W, :] = vh.T.astype(BF16)


def _qkv_call(x, shift, scale, w_qkv, cos, sin, tile):
    bsz, n, _ = x.shape
    use_rope = cos is not None
    vec = pl.BlockSpec((1, 1, D_MODEL), lambda b, i: (b, 0, 0))
    in_specs = [
        pl.BlockSpec((1, tile, D_MODEL), lambda b, i: (b, i, 0)),
        vec, vec,
        pl.BlockSpec((D_MODEL, QKV_W), lambda b, i: (0, 0)),
    ]
    args = [x, shift, scale, w_qkv]
    if use_rope:
        tab = pl.BlockSpec((tile, HEAD_W), lambda b, i: (i, 0))
        in_specs += [tab, tab]
        args += [cos, sin]
    row = pl.BlockSpec((1, tile, 512), lambda b, i: (b, i, 0))
    return pl.pallas_call(
        functools.partial(_qkv_kernel, use_rope=use_rope),
        grid=(bsz, n // tile),
        in_specs=in_specs,
        out_specs=[row, row, row, pl.BlockSpec((1, 512, tile), lambda b, i: (b, 0, i))],
        out_shape=[
            jax.ShapeDtypeStruct((bsz, n, 512), BF16),
            jax.ShapeDtypeStruct((bsz, n, 512), BF16),
            jax.ShapeDtypeStruct((bsz, n, 512), BF16),
            jax.ShapeDtypeStruct((bsz, 512, n), BF16),
        ],
        compiler_params=_cparams(("parallel", "parallel")),
        name="qkv_rope" if use_rope else "qkv_ctx",
    )(*args)


def _rest_kernel(x_ref, shift_ref, scale_ref, w_ref, o_ref):
    h = (x_ref[0] * (1.0 + scale_ref[0]) + shift_ref[0]).astype(BF16)
    o_ref[0] = _dot(h, w_ref[...]).astype(o_ref.dtype)


def _rest_call(x, shift, scale, w_rest, tile):
    bsz, n, _ = x.shape
    tn = REST_TN
    vec = pl.BlockSpec((1, 1, D_MODEL), lambda j, b, i: (b, 0, 0))
    return pl.pallas_call(
        _rest_kernel,
        grid=(REST_W // tn, bsz, n // tile),
        in_specs=[
            pl.BlockSpec((1, tile, D_MODEL), lambda j, b, i: (b, i, 0)),
            vec, vec,
            pl.BlockSpec((D_MODEL, tn), lambda j, b, i: (0, j)),
        ],
        out_specs=pl.BlockSpec((1, tile, tn), lambda j, b, i: (b, i, j)),
        out_shape=jax.ShapeDtypeStruct((bsz, n, REST_W), BF16),
        compiler_params=_cparams(("parallel", "parallel", "parallel")),
        name="in_proj_rest",
    )(x, shift, scale, w_rest)


def _attn_kernel(lam_ref, g_ref, q_ref, k12_ref, vt_ref, o_ref, sa_ref, sb_ref, *, tq):
    lv = lam_ref[...]
    lam_init = lv[4:5, 0:1]
    lam = (jnp.exp(jnp.sum(lv[0:1] * lv[1:2], axis=-1, keepdims=True))
           - jnp.exp(jnp.sum(lv[2:3] * lv[3:4], axis=-1, keepdims=True)) + lam_init)
    gain = g_ref[...] * (1.0 - lam_init)
    nk = vt_ref.shape[2]
    nq = q_ref.shape[1] // tq

    def scores_and_max(i, s_ref):
        q = q_ref[0, pl.ds(pl.multiple_of(i * tq, tq), tq), :]
        s = _dot_nt(k12_ref[0], q)
        s_ref[...] = s
        return (jnp.max(s[0:nk], axis=0, keepdims=True), jnp.max(s[nk:2 * nk], axis=0, keepdims=True))

    def softmax_pv(i, s_ref, m):
        p1 = jnp.exp2(s_ref[0:nk] - m[0])
        p2 = jnp.exp2(s_ref[nk:2 * nk] - m[1])
        r1 = 1.0 / jnp.sum(p1, axis=0, keepdims=True)
        r2 = lam / jnp.sum(p2, axis=0, keepdims=True)
        o1 = _dot(vt_ref[0], p1.astype(BF16))
        o2 = _dot(vt_ref[0], p2.astype(BF16))
        o = (o1 * r1 - o2 * r2).T
        y = o * lax.rsqrt(jnp.mean(o * o, axis=-1, keepdims=True) + LN_EPS)
        o_ref[0, pl.ds(pl.multiple_of(i * tq, tq), tq), :] = (y * gain).astype(o_ref.dtype)

    ma = scores_and_max(0, sa_ref)
    if nq == 1:
        softmax_pv(0, sa_ref, ma)
        return

    def pair(j, ma):
        mb = scores_and_max(2 * j + 1, sb_ref)
        softmax_pv(2 * j, sa_ref, ma)
        ma = scores_and_max(2 * j + 2, sa_ref)
        softmax_pv(2 * j + 1, sb_ref, mb)
        return ma

    ma = lax.fori_loop(0, nq // 2 - 1, pair, ma)
    mb = scores_and_max(nq - 1, sb_ref)
    softmax_pv(nq - 2, sa_ref, ma)
    softmax_pv(nq - 1, sb_ref, mb)


def _attn_call(lam_vecs, g, q, k12, vt, tq):
    bsz, n, _ = q.shape
    nk = vt.shape[2]
    nq = n // tq
    assert n % tq == 0 and (nq == 1 or nq % 2 == 0)
    return pl.pallas_call(
        functools.partial(_attn_kernel, tq=tq),
        grid=(bsz, ATT_HEADS),
        in_specs=[
            pl.BlockSpec((8, ATT_HEAD_DIM), lambda b, h: (0, 0)),
            pl.BlockSpec((1, HEAD_W), lambda b, h: (0, 0)),
            pl.BlockSpec((1, n, HEAD_W), lambda b, h: (b, 0, h)),
            pl.BlockSpec((1, 2 * nk, HEAD_W), lambda b, h: (b, 0, h)),
            pl.BlockSpec((1, HEAD_W, nk), lambda b, h: (b, h, 0)),
        ],
        out_specs=pl.BlockSpec((1, n, HEAD_W), lambda b, h: (b, 0, h)),
        out_shape=jax.ShapeDtypeStruct((bsz, n, 512), BF16),
        scratch_shapes=[pltpu.VMEM((2 * nk, tq), F32), pltpu.VMEM((2 * nk, tq), F32)],
        compiler_params=_cparams(("parallel", "parallel")),
        name="diff_attn",
    )(lam_vecs, g, q, k12, vt)


SUBLANES = 8


def _depthwise_taps(buf, shift_buf, w_ref, ntaps, t):
    first = HALO_ROWS - (ntaps - 1) // 2
    conv = jnp.zeros((t, buf.shape[1]), F32)
    for r in range(SUBLANES):
        taps = [k for k in range(ntaps) if (first + k) % SUBLANES == r]
        if not taps:
            continue
        src = buf
        if r:
            span = max((first + k) // SUBLANES for k in taps) * SUBLANES + t
            shift_buf[0:span, :] = buf[r:r + span, :]
            src = shift_buf
        for k in taps:
            a = (first + k) // SUBLANES * SUBLANES
            conv = conv + src[a:a + t, :] * w_ref[k:k + 1, :]
    return conv
def _mixer_kernel(z_ref, prev_ref, next_ref, yb_ref, x_ref, gate1_ref,
                  sgu_g_ref, sgu_b_ref, sgu_w_ref, sgu_bias_ref,
                  sconv_w_ref, cconv_w_ref, cconv_b_ref, cc_g_ref, cc_b_ref,
                  wbr_ref, wout_ref, ln_g_ref, ln_b_ref,
                  o_ref, u_buf, g_buf, shift_buf):
    i = pl.program_id(1)
    last = pl.num_programs(1) - 1
    t = x_ref.shape[1]
    has_prev = jnp.where(i > 0, 1.0, 0.0).astype(F32)
    has_next = jnp.where(i < last, 1.0, 0.0).astype(F32)

    def zs(col, width=BRANCH_W):
        return z_ref[0, :, col:col + width].astype(F32)

    def halo(ref, col):
        return ref[0, :, col:col + BRANCH_W].astype(F32)

    def glu(a, b):
        return a * _sigmoid(b)

    u_buf[0:HALO_ROWS, :] = halo(prev_ref, R_GC) * halo(prev_ref, R_XT) * has_prev
    u_buf[HALO_ROWS:HALO_ROWS + t, :] = zs(R_GC) * zs(R_XT)
    u_buf[HALO_ROWS + t:, :] = halo(next_ref, R_GC) * halo(next_ref, R_XT) * has_next
    y_c = zs(R_SB) * _depthwise_taps(u_buf, shift_buf, sconv_w_ref, SCONV_K, t)

    g_buf[0:HALO_ROWS, :] = glu(halo(prev_ref, R_GA), halo(prev_ref, R_GB)) * has_prev
    g_buf[HALO_ROWS:HALO_ROWS + t, :] = glu(zs(R_GA), zs(R_GB))
    g_buf[HALO_ROWS + t:, :] = glu(halo(next_ref, R_GA), halo(next_ref, R_GB)) * has_next
    conv = _depthwise_taps(g_buf, shift_buf, cconv_w_ref, CCONV_K, t)
    yd = _layer_norm(conv + cconv_b_ref[...], cc_g_ref[...], cc_b_ref[...])
    y_d = yd * _sigmoid(yd)

    def gelu(v):
        return 0.5 * v * (1.0 + lax.erf(v * np.float32(math.sqrt(0.5))))

    su = gelu(zs(R_SU))
    sv = _layer_norm(gelu(zs(R_SV)), sgu_g_ref[...], sgu_b_ref[...]).astype(BF16)
    rows = []
    for c in range(t // CHUNK):
        cols = []
        for g in range(SGU_GROUPS):
            vb = sv[c * CHUNK:(c + 1) * CHUNK, g * CHUNK:(g + 1) * CHUNK]
            cols.append(_dot(sgu_w_ref[g], vb))
        rows.append(jnp.concatenate(cols, axis=1) + sgu_bias_ref[...])
    y_a = su * jnp.concatenate(rows, axis=0)

    branches = (y_a.astype(BF16), yb_ref[0], y_c.astype(BF16), y_d.astype(BF16))
    mix = jnp.zeros((t, D_MODEL), F32)
    for k in range(N_BRANCH):
        gate = _sigmoid(zs(R_GATE + k * D_MODEL, D_MODEL))
        mix = mix + gate * _dot(branches[k], wbr_ref[k])
    upd = _dot(mix.astype(BF16), wout_ref[...])
    o_ref[0] = _layer_norm(DEEPNORM_ALPHA * x_ref[0] + gate1_ref[0] * upd,
                           ln_g_ref[...], ln_b_ref[...])


def _mixer_call(z, yb, x, gate1, lp, tile):
    bsz, n, _ = x.shape
    nh = n // HALO_ROWS
    hb = tile // HALO_ROWS

    def const(shape):
        return pl.BlockSpec(shape, lambda b, i: (0,) * len(shape))

    in_specs = [
        pl.BlockSpec((1, tile, REST_W), lambda b, i: (b, i, 0)),
        pl.BlockSpec((1, HALO_ROWS, HALO_W), lambda b, i: (b, jnp.maximum(i * hb - 1, 0), 0)),
        pl.BlockSpec((1, HALO_ROWS, HALO_W), lambda b, i: (b, jnp.minimum((i + 1) * hb, nh - 1), 0)),
        pl.BlockSpec((1, tile, BRANCH_W), lambda b, i: (b, i, 0)),
        pl.BlockSpec((1, tile, D_MODEL), lambda b, i: (b, i, 0)),
        pl.BlockSpec((1, 1, D_MODEL), lambda b, i: (b, 0, 0)),
        const((1, BRANCH_W)), const((1, BRANCH_W)),
        const((SGU_GROUPS, CHUNK, CHUNK)), const((CHUNK, BRANCH_W)),
        const((8, BRANCH_W)), const((32, BRANCH_W)),
        const((1, BRANCH_W)), const((1, BRANCH_W)), const((1, BRANCH_W)),
        const((N_BRANCH, BRANCH_W, D_MODEL)), const((D_MODEL, D_MODEL)),
        const((1, D_MODEL)), const((1, D_MODEL)),
    ]
    return pl.pallas_call(
        _mixer_kernel,
        grid=(bsz, n // tile),
        in_specs=in_specs,
        out_specs=pl.BlockSpec((1, tile, D_MODEL), lambda b, i: (b, i, 0)),
        out_shape=jax.ShapeDtypeStruct((bsz, n, D_MODEL), F32),
        scratch_shapes=[pltpu.VMEM((tile + 2 * HALO_ROWS, BRANCH_W), F32)] * 3,
        compiler_params=_cparams(("parallel", "parallel")),
        name="mixer",
    )(z, z, z, yb, x, gate1,
      lp["sgu_ln_g"], lp["sgu_ln_b"], lp["sgu_w"], lp["sgu_bias"],
      lp["sconv_w"], lp["cconv_w"], lp["cconv_b"], lp["cconv_ln_g"], lp["cconv_ln_b"],
      lp["w_branch"], lp["w_out"], lp["ln1_g"], lp["ln1_b"])


def _ffn_kernel(x_ref, shift_ref, scale_ref, gate_ref, wup_ref, wdn_ref, ln_g_ref, ln_b_ref, o_ref):
    x = x_ref[0]
    h = (x * (1.0 + scale_ref[0]) + shift_ref[0]).astype(BF16)
    acc = jnp.zeros(x.shape, F32)
    step = D_MODEL
    for c in range(D_FF // step):
        up = jnp.maximum(_dot(h, wup_ref[:, c * step:(c + 1) * step]), 0.0)
        acc = acc + _dot((up * up).astype(BF16), wdn_ref[c * step:(c + 1) * step, :])
    o_ref[0] = _layer_norm(DEEPNORM_ALPHA * x + gate_ref[0] * acc, ln_g_ref[...], ln_b_ref[...])


def _ffn_call(x, shift, scale, gate, lp, tile):
    bsz, n, _ = x.shape
    vec = pl.BlockSpec((1, 1, D_MODEL), lambda b, i: (b, 0, 0))
    row = pl.BlockSpec((1, tile, D_MODEL), lambda b, i: (b, i, 0))
    cvec = pl.BlockSpec((1, D_MODEL), lambda b, i: (0, 0))
    return pl.pallas_call(
        _ffn_kernel,
        grid=(bsz, n // tile),
        in_specs=[row, vec, vec, vec,
                  pl.BlockSpec((D_MODEL, D_FF), lambda b, i: (0, 0)),
                  pl.BlockSpec((D_FF, D_MODEL), lambda b, i: (0, 0)),
                  cvec, cvec],
        out_specs=row,
        out_shape=jax.ShapeDtypeStruct((bsz, n, D_MODEL), F32),
        compiler_params=_cparams(("parallel", "parallel")),
        name="ffn",
    )(x, shift, scale, gate, lp["w_up"], lp["w_down"], lp["ln2_g"], lp["ln2_b"])


_OFF_Q, _OFF_K, _OFF_V, _OFF_SGU, _OFF_SCONV, _OFF_CCONV, _OFF_GATE = 0, 512, 1024, 1536, 2560, 4096, 5120


def _qk_cols(w):
    lead = w.shape[:-1]
    w = w.reshape(lead + (2, ATT_HEADS, ATT_HEAD_DIM // 2, 2))
    return jnp.moveaxis(w, (-4, -3, -2, -1), (-2, -4, -1, -3)).reshape(lead + (512,))


def _split_w_in(w_in):
    def cols(a, b):
        return w_in[..., a:b]

    w_qkv = jnp.concatenate([_qk_cols(cols(_OFF_Q, _OFF_K)), _qk_cols(cols(_OFF_K, _OFF_V)),
                             cols(_OFF_V, _OFF_SGU)], axis=-1)
    w_rest = jnp.concatenate([
        cols(_OFF_SCONV + 512, _OFF_SCONV + 1536),
        cols(_OFF_CCONV, _OFF_CCONV + 1024),
        cols(_OFF_SCONV, _OFF_SCONV + 512),
        cols(_OFF_SGU, _OFF_SGU + 1024),
        cols(_OFF_GATE, _OFF_GATE + 4096),
    ], axis=-1)
    return w_qkv.astype(BF16), w_rest.astype(BF16)


def _rope_tables(n):
    rows = n // GRID_W
    row = jnp.repeat(jnp.arange(rows, dtype=F32), GRID_W)
    col = jnp.tile(jnp.arange(GRID_W, dtype=F32), rows)
    inv_freq = ROPE_THETA ** (-jnp.arange(ROPE_AXIS_PAIRS, dtype=F32) / ROPE_AXIS_PAIRS)
    ang = jnp.concatenate([row[:, None] * inv_freq, col[:, None] * inv_freq], axis=-1)
    cos, sin = jnp.cos(ang), jnp.sin(ang)
    return jnp.tile(cos, (1, 4)), jnp.concatenate([-sin, -sin, sin, sin], axis=-1)


def _pad_rows(w, rows):
    return jnp.pad(w, ((0, rows - w.shape[0]), (0, 0)))


def kernel(x, c, ctx, c_ctx, w_ada, b_ada, w_in, lam_q1, lam_k1, lam_q2, lam_k2, attn_subln_g, sgu_ln_g, sgu_ln_b, sgu_w, sgu_b, sconv_w, cconv_w, cconv_b, cconv_ln_g, cconv_ln_b, w_branch, w_out, ln1_g, ln1_b, w_up, w_down, ln2_g, ln2_b):
    bsz, n_lat, _ = x.shape
    n_ctx = ctx.shape[1]
    depth = w_in.shape[0]
    cos, sin = _rope_tables(n_lat)

    rows = ((bsz + 1 + 7) // 8) * 8
    cc = jnp.zeros((rows, D_MODEL), F32).at[:bsz].set(c).at[bsz].set(c_ctx)
    mod = _ada_call(cc, w_ada, b_ada)

    w_qkv_all, w_rest_all = _split_w_in(w_in)
    sgu_w_bf, w_branch_bf, w_out_bf = sgu_w.astype(BF16), w_branch.astype(BF16), w_out.astype(BF16)
    w_up_bf, w_down_bf = w_up.astype(BF16), w_down.astype(BF16)

    lat_tile = min(512, n_lat)
    tq = min(256, n_lat)
    xl = x
    xc = ctx.reshape(1, bsz * n_ctx, D_MODEL)
    for l in range(depth):
        lam_init = 0.8 - 0.6 * math.exp(-0.3 * l)
        lam_vecs = jnp.concatenate(
            [lam_q1[l][None], lam_k1[l][None], lam_q2[l][None], lam_k2[l][None],
             jnp.full((4, ATT_HEAD_DIM), lam_init, F32)], axis=0)
        g_attn = attn_subln_g[l][None]
        ml = [m[:, None, :] for m in jnp.split(mod[l, :bsz], 6, axis=-1)]
        mc = [m[None] for m in jnp.split(mod[l, bsz:bsz + 1], 6, axis=-1)]
        w_qkv = w_qkv_all[l]
        w_rest = w_rest_all[l]
        lp = dict(
            sgu_ln_g=sgu_ln_g[l][None], sgu_ln_b=sgu_ln_b[l][None],
            sgu_w=sgu_w_bf[l],
            sgu_bias=jnp.repeat(sgu_b[l].T, CHUNK, axis=1),
            sconv_w=_pad_rows(sconv_w[l], 8), cconv_w=_pad_rows(cconv_w[l], 32),
            cconv_b=cconv_b[l][None], cconv_ln_g=cconv_ln_g[l][None], cconv_ln_b=cconv_ln_b[l][None],
            w_branch=w_branch_bf[l], w_out=w_out_bf[l],
            ln1_g=ln1_g[l][None], ln1_b=ln1_b[l][None],
            w_up=w_up_bf[l], w_down=w_down_bf[l],
            ln2_g=ln2_g[l][None], ln2_b=ln2_b[l][None],
        )
        last = l == depth - 1

        qc, k1c, k2c, vtc = _qkv_call(xc, mc[0], mc[1], w_qkv, None, None, min(1024, bsz * n_ctx))
        k1c = k1c.reshape(bsz, n_ctx, 512)
        k2c = k2c.reshape(bsz, n_ctx, 512)
        vtc = vtc.reshape(512, bsz, n_ctx).transpose(1, 0, 2)
        if not last:
            zc = _rest_call(xc, mc[0], mc[1], w_rest, min(1024, bsz * n_ctx))
            ybc = _attn_call(lam_vecs, g_attn, qc.reshape(bsz, n_ctx, 512),
                             jnp.concatenate([k1c, k2c], axis=1), vtc, n_ctx)
            xcb = xc.reshape(bsz, n_ctx, D_MODEL)
            gate1c = jnp.broadcast_to(mc[2], (bsz, 1, D_MODEL))
            xm = _mixer_call(zc.reshape(bsz, n_ctx, REST_W), ybc, xcb, gate1c, lp, n_ctx)
            xc_next = _ffn_call(xm.reshape(1, bsz * n_ctx, D_MODEL), mc[3], mc[4], mc[5], lp,
                                min(512, bsz * n_ctx))

        ql, k1l, k2l, vtl = _qkv_call(xl, ml[0], ml[1], w_qkv, cos, sin, lat_tile)
        zl = _rest_call(xl, ml[0], ml[1], w_rest, min(1024, n_lat))
        k12 = jnp.concatenate([k1c, k1l, k2c, k2l], axis=1)
        vt = jnp.concatenate([vtc, vtl], axis=2)
        ybl = _attn_call(lam_vecs, g_attn, ql, k12, vt, tq)
        xm = _mixer_call(zl, ybl, xl, ml[2], lp, lat_tile)
        xl = _ffn_call(xm, ml[3], ml[4], ml[5], lp, lat_tile)
        if not last:
            xc = xc_next
    return xl
```
